```python
import jax, jax.numpy as jnp
from jax import lax
import numpy as np

D_MODEL = 1024
BATCH = 4
SEQ = 8192
DEPTH = 1
DEC_BATCH = 16
DEC_SEQ = 16
PAST_LEN = 4096

CHUNK = 64
H_RET = 8
RET_DK = 64
RET_DV = 64
RET_THETA = 10000.0
N_Q_HEADS = 8
N_KV_HEADS = 2
HEAD_DIM = 64
Q_PER_KV = N_Q_HEADS // N_KV_HEADS
WINDOW = 128
WIN_CHUNKS = WINDOW // CHUNK
ROPE_THETA = 500000.0
ROPE_DIM = HEAD_DIM // 4
MIX_WIDTH = H_RET * RET_DV + N_Q_HEADS * HEAD_DIM
IN_WIDTH = 2 * H_RET * RET_DK + 2 * H_RET * RET_DV + N_Q_HEADS * HEAD_DIM + 2 * N_KV_HEADS * HEAD_DIM
PEER_HEADS = 8
PEER_NKEYS = 128
PEER_EXPERTS = PEER_NKEYS * PEER_NKEYS
PEER_TOPK = 16
PEER_DKH = 128
PEER_BLOCK = 128
NORM_EPS = 1e-6
NEG_INF = -1e30

kernel_name = 'hybrid_retention_swa_peer_step'


def _rmsnorm(x, g):
    xf = x.astype(jnp.float32)
    y = xf * lax.rsqrt(jnp.mean(xf * xf, axis=-1, keepdims=True) + NORM_EPS)
    return (y * g.astype(jnp.float32)).astype(x.dtype)


def _rope(x, pos, rot_dim, theta):
    half = rot_dim // 2
    freqs = theta ** (-jnp.arange(half, dtype=jnp.float32) / half)
    ang = pos[:, None] * freqs[None, :]
    cos = jnp.cos(ang)[:, None, :]
    sin = jnp.sin(ang)[:, None, :]
    xf = x.astype(jnp.float32)
    x1 = xf[..., :half]
    x2 = xf[..., half:rot_dim]
    out = jnp.concatenate([x1 * cos - x2 * sin, x2 * cos + x1 * sin, xf[..., rot_dim:]], axis=-1)
    return out.astype(x.dtype)


def _in_proj(h, w_in, pos):
    B, T, _ = h.shape
    sizes = (H_RET * RET_DK, H_RET * RET_DK, H_RET * RET_DV, H_RET * RET_DV,
             N_Q_HEADS * HEAD_DIM, N_KV_HEADS * HEAD_DIM, N_KV_HEADS * HEAD_DIM)
    offsets = [int(v) for v in np.cumsum(sizes)[:-1]]
    z = h @ w_in
    qr, kr, vr, gr, qa, ka, va = jnp.split(z, offsets, axis=-1)
    qr = _rope(qr.reshape(B, T, H_RET, RET_DK), pos, RET_DK, RET_THETA)
    kr = _rope(kr.reshape(B, T, H_RET, RET_DK), pos, RET_DK, RET_THETA) * RET_DK ** -0.5
    vr = vr.reshape(B, T, H_RET, RET_DV)
    qa = _rope(qa.reshape(B, T, N_Q_HEADS, HEAD_DIM), pos, ROPE_DIM, ROPE_THETA)
    ka = _rope(ka.reshape(B, T, N_KV_HEADS, HEAD_DIM), pos, ROPE_DIM, ROPE_THETA)
    va = va.reshape(B, T, N_KV_HEADS, HEAD_DIM)
    return qr, kr, vr, gr, qa, ka, va


def _retention(q, k, v, s0, chunk):
    B, T, H, DK = q.shape
    DV = v.shape[-1]
    nc = T // chunk
    qc = q.astype(jnp.float32).reshape(B, nc, chunk, H, DK)
    kc = k.astype(jnp.float32).reshape(B, nc, chunk, H, DK)
    vc = v.astype(jnp.float32).reshape(B, nc, chunk, H, DV)
    log_g = jnp.log(1.0 - jnp.exp2(-5.0 - jnp.arange(H, dtype=jnp.float32)))
    idx = jnp.arange(chunk, dtype=jnp.float32)
    dist = jnp.abs(idx[:, None] - idx[None, :])
    decay_intra = jnp.exp(log_g[:, None, None] * dist)
    decay_q = jnp.exp(log_g[:, None] * (idx[None, :] + 1.0))
    decay_k = jnp.exp(log_g[:, None] * (chunk - 1.0 - idx[None, :]))
    decay_chunk = jnp.exp(log_g * chunk)
    scores = jnp.einsum('bnihd,bnjhd->bnhij', qc, kc) * decay_intra
    intra = jnp.einsum('bnhij,bnjhe->bnihe', scores, vc)
    kv = jnp.einsum('bnjhd,hj,bnjhe->nbhde', kc, decay_k, vc)

    def step(s, kv_n):
        return decay_chunk[None, :, None, None] * s + kv_n, s

    s_last, s_prev = lax.scan(step, s0.astype(jnp.float32), kv)
    inter = jnp.einsum('bnihd,nbhde,hi->bnihe', qc, s_prev, decay_q)
    o = (intra + inter).reshape(B, T, H, DV)
    return o, s_last


def _sink_attention(qb, kb, vb, valid, sinks):
    s = jnp.einsum('bnckgd,bnlkd->bnkgcl', qb, kb).astype(jnp.float32) * HEAD_DIM ** -0.5
    if valid is not None:
        s = jnp.where(valid[None, :, None, None, None, :], s, NEG_INF)
    sink = sinks.astype(jnp.float32).reshape(N_KV_HEADS, Q_PER_KV)[None, None, :, :, None, None]
    m = jnp.maximum(jnp.max(s, axis=-1, keepdims=True), sink)
    p = jnp.exp(s - m)
    denom = jnp.sum(p, axis=-1, keepdims=True) + jnp.exp(sink - m)
    return jnp.einsum('bnkgcl,bnlkd->bnckgd', p / denom, vb.astype(jnp.float32))


def _swa_prompt(q, k, v, sinks):
    B, T = q.shape[:2]
    nb = T // CHUNK
    qb = q.reshape(B, nb, CHUNK, N_KV_HEADS, Q_PER_KV, HEAD_DIM)
    pad = ((0, 0), (WINDOW, 0), (0, 0), (0, 0))
    kc = jnp.pad(k, pad).reshape(B, nb + WIN_CHUNKS, CHUNK, N_KV_HEADS, HEAD_DIM)
    vc = jnp.pad(v, pad).reshape(B, nb + WIN_CHUNKS, CHUNK, N_KV_HEADS, HEAD_DIM)
    kb = jnp.concatenate([kc[:, i:i + nb] for i in range(WIN_CHUNKS + 1)], axis=2)
    vb = jnp.concatenate([vc[:, i:i + nb] for i in range(WIN_CHUNKS + 1)], axis=2)
    key_pos = (jnp.arange(nb)[:, None] - WIN_CHUNKS) * CHUNK + jnp.arange((WIN_CHUNKS + 1) * CHUNK)[None, :]
    o = _sink_attention(qb, kb, vb, key_pos >= 0, sinks)
    return o.reshape(B, T, N_Q_HEADS * HEAD_DIM)


def _swa_sample(q, k, v, cache_k, cache_v, sinks):
    B, T = q.shape[:2]
    qb = q.reshape(B, 1, T, N_KV_HEADS, Q_PER_KV, HEAD_DIM)
    k_ctx = jnp.concatenate([cache_k.astype(k.dtype), k], axis=1)
    v_ctx = jnp.concatenate([cache_v.astype(v.dtype), v], axis=1)
    o = _sink_attention(qb, k_ctx[:, None], v_ctx[:, None], None, sinks)
    return o.reshape(B, T, N_Q_HEADS * HEAD_DIM), k_ctx[:, -WINDOW:], v_ctx[:, -WINDOW:]


def _merge(o_ret, gr, o_att, w_out, dtype):
    B, T = o_ret.shape[:2]
    mu = jnp.mean(o_ret, axis=-1, keepdims=True)
    var = jnp.mean(jnp.square(o_ret - mu), axis=-1, keepdims=True)
    o_r = ((o_ret - mu) * lax.rsqrt(var + NORM_EPS)).reshape(B, T, H_RET * RET_DV)
    o_r = jax.nn.silu(gr.astype(jnp.float32)) * o_r
    cat = jnp.concatenate([o_r, o_att], axis=-1).astype(dtype)
    return cat @ w_out


def _peer_tokens(h, w_q, keys, u_tab, v_tab):
    T = h.shape[0]
    q = (h @ w_q).reshape(T, PEER_HEADS, 2, PEER_DKH)
    s = jnp.einsum('tpsd,psnd->tpsn', q, keys).astype(jnp.float32)
    s1, i1 = lax.top_k(s[:, :, 0], PEER_TOPK)
    s2, i2 = lax.top_k(s[:, :, 1], PEER_TOPK)
    cand = (s1[..., :, None] + s2[..., None, :]).reshape(T, PEER_HEADS, PEER_TOPK * PEER_TOPK)
    cidx = (i1[..., :, None] * PEER_NKEYS + i2[..., None, :]).reshape(T, PEER_HEADS, PEER_TOPK * PEER_TOPK)
    sc, sel = lax.top_k(cand, PEER_TOPK)
    eidx = jnp.take_along_axis(cidx, sel, axis=-1)
    gate = jax.nn.softmax(sc, axis=-1)
    u = u_tab[eidx]
    act = jax.nn.gelu(jnp.einsum('td,tpkd->tpk', h, u).astype(jnp.float32))
    return jnp.einsum('tpk,tpkd->td', (gate * act).astype(h.dtype), v_tab[eidx])


def _peer(h, w_q, keys, u_tab, v_tab):
    shape = h.shape
    flat = h.reshape(-1, shape[-1])
    n = flat.shape[0]
    if n % PEER_BLOCK == 0 and n > PEER_BLOCK:
        blocks = flat.reshape(n // PEER_BLOCK, PEER_BLOCK, shape[-1])
        out = lax.map(lambda hb: _peer_tokens(hb, w_q, keys, u_tab, v_tab), blocks).reshape(n, shape[-1])
    else:
        out = _peer_tokens(flat, w_q, keys, u_tab, v_tab)
    return out.reshape(shape)


def setup_inputs(seed: int = 0) -> dict:
    key = jax.random.key(seed)
    ks = jax.random.split(key, 16)

    def nrm(k, shape, scale):
        return jax.random.normal(k, shape, jnp.float32) * scale

    return {
        'x_prompt': nrm(ks[0], (BATCH, SEQ, D_MODEL), 1.0),
        'x_sample': nrm(ks[1], (DEC_BATCH, DEC_SEQ, D_MODEL), 1.0),
        'state_ret': nrm(ks[2], (DEPTH, DEC_BATCH, H_RET, RET_DK, RET_DV), 0.5),
        'cache_swa_k': nrm(ks[3], (DEPTH, DEC_BATCH, WINDOW, N_KV_HEADS, HEAD_DIM), 1.0),
        'cache_swa_v': nrm(ks[4], (DEPTH, DEC_BATCH, WINDOW, N_KV_HEADS, HEAD_DIM), 1.0),
        'norm_mix': 1.0 + nrm(ks[5], (DEPTH, D_MODEL), 0.02),
        'w_in': nrm(ks[6], (DEPTH, D_MODEL, IN_WIDTH), D_MODEL ** -0.5),
        'attn_sinks': nrm(ks[7], (DEPTH, N_Q_HEADS), 0.5),
        'w_out': nrm(ks[8], (DEPTH, MIX_WIDTH, D_MODEL), MIX_WIDTH ** -0.5),
        'norm_ffn': 1.0 + nrm(ks[9], (DEPTH, D_MODEL), 0.02),
        'peer_w_q': nrm(ks[10], (DEPTH, D_MODEL, PEER_HEADS * 2 * PEER_DKH), D_MODEL ** -0.5),
        'peer_keys': nrm(ks[11], (DEPTH, PEER_HEADS, 2, PEER_NKEYS, PEER_DKH), PEER_DKH ** -0.5),
        'peer_u': nrm(ks[12], (DEPTH, PEER_EXPERTS, D_MODEL), D_MODEL ** -0.5),
        'peer_v': nrm(ks[13], (DEPTH, PEER_EXPERTS, D_MODEL), 0.5),
        'norm_final': 1.0 + nrm(ks[14], (D_MODEL,), 0.02),
    }


def reference(x_prompt, x_sample, state_ret, cache_swa_k, cache_swa_v, norm_mix, w_in, attn_sinks,
              w_out, norm_ffn, peer_w_q, peer_keys, peer_u, peer_v, norm_final):
    xp = x_prompt
    xs = x_sample
    Bp, Tp = xp.shape[:2]
    Bs, Ts = xs.shape[:2]
    pos_p = jnp.arange(Tp, dtype=jnp.float32)
    pos_s = jnp.arange(Ts, dtype=jnp.float32) + PAST_LEN
    ret_p, k_p, v_p, ret_s, k_s, v_s = [], [], [], [], [], []
    for l in range(DEPTH):
        h = _rmsnorm(xp, norm_mix[l])
        qr, kr, vr, gr, qa, ka, va = _in_proj(h, w_in[l], pos_p)
        s0 = jnp.zeros((Bp, H_RET, RET_DK, RET_DV), jnp.float32)
        o_ret, s_last = _retention(qr, kr, vr, s0, CHUNK)
        o_att = _swa_prompt(qa, ka, va, attn_sinks[l])
        xp = xp + _merge(o_ret, gr, o_att, w_out[l], xp.dtype)
        xp = xp + _peer(_rmsnorm(xp, norm_ffn[l]), peer_w_q[l], peer_keys[l], peer_u[l], peer_v[l])
        ret_p.append(s_last)
        k_p.append(ka[:, -WINDOW:])
        v_p.append(va[:, -WINDOW:])
        h = _rmsnorm(xs, norm_mix[l])
        qr, kr, vr, gr, qa, ka, va = _in_proj(h, w_in[l], pos_s)
        o_ret, s_last = _retention(qr, kr, vr, state_ret[l], Ts)
        o_att, k_buf, v_buf = _swa_sample(qa, ka, va, cache_swa_k[l], cache_swa_v[l], attn_sinks[l])
        xs = xs + _merge(o_ret, gr, o_att, w_out[l], xs.dtype)
        xs = xs + _peer(_rmsnorm(xs, norm_ffn[l]), peer_w_q[l], peer_keys[l], peer_u[l], peer_v[l])
        ret_s.append(s_last)
        k_s.append(k_buf)
        v_s.append(v_buf)
    y_prompt = _rmsnorm(xp, norm_final)
    y_sample = _rmsnorm(xs, norm_final)
    new_state_ret_prompt = jnp.stack(ret_p, axis=0)
    new_cache_swa_k_prompt = jnp.stack(k_p, axis=0)
    new_cache_swa_v_prompt = jnp.stack(v_p, axis=0)
    new_state_ret_sample = jnp.stack(ret_s, axis=0)
    new_cache_swa_k_sample = jnp.stack(k_s, axis=0)
    new_cache_swa_v_sample = jnp.stack(v_s, axis=0)
    return (y_prompt, y_sample, new_state_ret_prompt, new_cache_swa_k_prompt, new_cache_swa_v_prompt,
            new_state_ret_sample, new_cache_swa_k_sample, new_cache_swa_v_sample)
```

```python
import functools

import numpy as np
import jax
import jax.numpy as jnp
from jax import lax
from jax.experimental import pallas as pl
from jax.experimental.pallas import tpu as pltpu

D_MODEL = 1024
CHUNK = 64
H_RET = 8
RET_DK = 64
RET_THETA = 10000.0
N_Q_HEADS = 8
N_KV_HEADS = 2
HEAD_DIM = 64
WINDOW = 128
ROPE_THETA = 500000.0
ROPE_DIM = HEAD_DIM // 4
PEER_HEADS = 8
PEER_NKEYS = 128
PEER_EXPERTS = PEER_NKEYS * PEER_NKEYS
PEER_TOPK = 16
PEER_DKH = 128
NORM_EPS = 1e-6
NEG_INF = -1e30
PAST_LEN = 4096

LANES = 128
SUBLANES = 8
VMEM_LIMIT_TABLE = 56 * 1024 * 1024

N_PAIRS = H_RET // 2
OFF_QR, OFF_KR, OFF_VR, OFF_GR = 0, 512, 1024, 1536
OFF_QA, OFF_KA, OFF_VA = 2048, 2560, 2688
IN_WIDTH = 2816
ROWS_PER_EXPERT = D_MODEL // 2 // LANES
SEL = PEER_HEADS * PEER_TOPK

MIX_BLOCK = 256
ROUTE_BLOCK = 128
GATHER_BLOCK = 128


def _bf(x):
    return x.astype(jnp.bfloat16)


def _dot(a, b):
    return jnp.dot(a, b, preferred_element_type=jnp.float32)


def _dot_nt(a, b):
    return lax.dot_general(a, b, (((1,), (1,)), ((), ())), preferred_element_type=jnp.float32)


def _dot_tn(a, b):
    return lax.dot_general(a, b, (((0,), (0,)), ((), ())), preferred_element_type=jnp.float32)


def _dot_split(a, m_bf):
    hi = _bf(a)
    lo = _bf(a - hi.astype(jnp.float32))
    return _dot(hi, m_bf) + _dot(lo, m_bf)


def _rope_cols(x, cos, sin, half):
    t = x.shape[0]
    lane = lax.broadcasted_iota(jnp.int32, (t, LANES), 1)
    first = (lane & (HEAD_DIM - 1)) < half
    cols = []
    for c in range(x.shape[1] // LANES):
        xs = x[:, c * LANES:(c + 1) * LANES]
        partner = jnp.where(first, pltpu.roll(xs, LANES - half, 1), pltpu.roll(xs, half, 1))
        cols.append(xs * cos + partner * sin)
    return cols


def _mixer_kernel(prompt, tb, *refs):
    if prompt:
        (sink_ref, x_ref, g_ref, win_ref, wout_ref, cr_ref, sr_ref, ca_ref, sa_ref, dmat_ref,
         dq_ref, dk_ref, gdm_ref, bias_ref,
         x1_ref, s_out_ref, kc_ref, vc_ref, s_scr, kprev_scr, vprev_scr) = refs
    else:
        (sink_ref, x_ref, g_ref, win_ref, wout_ref, cr_ref, sr_ref, ca_ref, sa_ref, dmat_ref,
         dq_ref, dk_ref, gdm_ref, bias_ref, s0_ref, ck_ref, cv_ref,
         x1_ref, s_out_ref, kc_ref, vc_ref) = refs

    if prompt:
        @pl.when(pl.program_id(1) == 0)
        def _():
            s_scr[...] = jnp.zeros_like(s_scr)
            kprev_scr[...] = jnp.zeros_like(kprev_scr)
            vprev_scr[...] = jnp.zeros_like(vprev_scr)
        state_ref = s_scr
        kprev = kprev_scr[...]
        vprev = vprev_scr[...]
    else:
        state_ref = s0_ref.at[0]
        kprev = ck_ref[0]
        vprev = cv_ref[0]

    x = x_ref[0]
    ms = jnp.mean(x * x, axis=-1, keepdims=True)
    h = _bf(x * lax.rsqrt(ms + NORM_EPS) * g_ref[...])

    def proj(off, width):
        return _dot(h, win_ref[:, off:off + width])

    lane = lax.broadcasted_iota(jnp.int32, (tb, LANES), 1)
    half_masks = (lane < HEAD_DIM, lane >= HEAD_DIM)
    row = lax.broadcasted_iota(jnp.int32, (LANES, LANES), 0)
    col = lax.broadcasted_iota(jnp.int32, (LANES, LANES), 1)
    same_head = (row < HEAD_DIM) == (col < HEAD_DIM)
    group_mean = _bf(jnp.where(same_head, 1.0 / HEAD_DIM, 0.0))

    cr, sr = cr_ref[...], sr_ref[...]
    q_cols = _rope_cols(proj(OFF_QR, 512), cr, sr, RET_DK // 2)
    k_cols = _rope_cols(proj(OFF_KR, 512), cr, sr, RET_DK // 2)
    v_all = proj(OFF_VR, 512)
    g_all = proj(OFF_GR, 512)
    merged = []
    rows = max(tb, LANES)

    def pad_rows(a):
        if rows == tb:
            return a
        return jnp.concatenate([a, jnp.zeros((rows - tb, a.shape[1]), a.dtype)], axis=0)

    for i in range(N_PAIRS):
        q = q_cols[i]
        k = pad_rows(k_cols[i] * (RET_DK ** -0.5))
        v = pad_rows(v_all[:, i * LANES:(i + 1) * LANES])
        kb = _bf(k)
        o = jnp.zeros((tb, LANES), jnp.float32)
        row_half = (lax.broadcasted_iota(jnp.int32, (rows, LANES), 1) < HEAD_DIM,
                    lax.broadcasted_iota(jnp.int32, (rows, LANES), 1) >= HEAD_DIM)
        for a in range(2):
            qm = _bf(jnp.where(half_masks[a], q, 0.0))
            sc = _dot_nt(qm, kb) * dmat_ref[2 * i + a]
            vm = _bf(jnp.where(row_half[a], v, 0.0))
            o = o + _dot(_bf(sc), vm)
        s_prev = state_ref[i]
        o = o + _dot(_bf(q), _bf(s_prev)) * dq_ref[:, i * LANES:(i + 1) * LANES]
        kd = k * pad_rows(dk_ref[:, i * LANES:(i + 1) * LANES])
        kv = _dot(_bf(kd.T), _bf(v))
        s_new = gdm_ref[i] * s_prev + jnp.where(same_head, kv, 0.0)
        if prompt:
            s_scr[i] = s_new
        s_out_ref[0, i] = s_new
        mu = _dot_split(o, group_mean)
        c = o - mu
        var = _dot_split(c * c, group_mean)
        on = c * lax.rsqrt(var + NORM_EPS)
        g = g_all[:, i * LANES:(i + 1) * LANES]
        merged.append(_bf(g * jax.nn.sigmoid(g) * on))

    ca, sa = ca_ref[...], sa_ref[...]
    qa_cols = _rope_cols(proj(OFF_QA, 512), ca, sa, ROPE_DIM // 2)
    ka = _rope_cols(proj(OFF_KA, 128), ca, sa, ROPE_DIM // 2)[0]
    va = proj(OFF_VA, 128)
    ctx = bias_ref.shape[2]
    tail = [jnp.zeros((ctx - WINDOW - tb, LANES), jnp.float32)] if ctx > WINDOW + tb else []
    k_ctx = jnp.concatenate([kprev, ka] + tail, axis=0)
    v_ctx = jnp.concatenate([vprev, va] + tail, axis=0)
    new_k = k_ctx[tb:tb + WINDOW]
    new_v = v_ctx[tb:tb + WINDOW]
    if prompt:
        kprev_scr[...] = new_k
        vprev_scr[...] = new_v
    kc_ref[0] = new_k
    vc_ref[0] = new_v
    k_both = (_bf(k_ctx), _bf(pltpu.roll(k_ctx, HEAD_DIM, 1)))
    v_sw = pltpu.roll(v_ctx, HEAD_DIM, 1)
    lane_c = lax.broadcasted_iota(jnp.int32, (ctx, LANES), 1)
    ctx_half = (lane_c < HEAD_DIM, lane_c >= HEAD_DIM)
    bias = bias_ref[0]
    for i in range(N_PAIRS):
        kvh = (2 * i) // (N_Q_HEADS // N_KV_HEADS)
        q = qa_cols[i]
        res = []
        for a in range(2):
            qm = _bf(jnp.where(half_masks[a], q, 0.0))
            kk = k_both[0] if kvh == a else k_both[1]
            s = _dot_nt(qm, kk) * (HEAD_DIM ** -0.5) + bias
            m = jnp.maximum(jnp.max(s, axis=-1, keepdims=True), sink_ref[2 * i + a])
            p = jnp.exp(s - m)
            vv = v_ctx if kvh == a else v_sw
            vext = _bf(jnp.where(ctx_half[a], vv, 1.0))
            r = _dot(_bf(p), vext)
            esink = jnp.exp(sink_ref[2 * i + a] - m)
            res.append(r + jnp.where(half_masks[a], 0.0, esink))
        num = jnp.where(half_masks[0], res[0], res[1])
        den = pltpu.roll(jnp.where(half_masks[0], res[1], res[0]), HEAD_DIM, 1)
        merged.append(_bf(num / den))

    cat = jnp.concatenate(merged, axis=1)
    x1_ref[0] = x + _dot(cat, wout_ref[...])


def _retention_tables(tb, chunk):
    hh = jnp.arange(H_RET, dtype=jnp.float32)
    log_g = jnp.log(1.0 - jnp.exp2(-5.0 - hh))
    idx = jnp.arange(tb, dtype=jnp.float32)
    dist = jnp.abs(idx[:, None] - idx[None, :])
    ci = jnp.arange(tb) // chunk
    visible = ci[None, :] <= ci[:, None]
    dmat = jnp.where(visible[None], jnp.exp(log_g[:, None, None] * dist[None]), 0.0)
    dq = jnp.exp(log_g[None, :] * (idx[:, None] + 1.0))
    dk = jnp.exp(log_g[None, :] * (tb - 1.0 - idx[:, None]))
    dq = jnp.repeat(dq, RET_DK, axis=1)
    dk = jnp.repeat(dk, RET_DK, axis=1)
    gd = jnp.repeat(jnp.exp(log_g * tb), RET_DK).reshape(N_PAIRS, LANES)
    r = jnp.arange(LANES) // HEAD_DIM
    blockdiag = (r[:, None] == r[None, :])
    gdm = jnp.where(blockdiag[None], gd[:, :, None], 0.0)
    rows = max(tb, LANES)
    dmat = jnp.pad(dmat, ((0, 0), (0, 0), (0, rows - tb)))
    return dmat, dq, dk, gdm


def _rope_tables(pos, half, theta):
    freqs = theta ** (-jnp.arange(half, dtype=jnp.float32) / half)
    ang = pos[:, None] * freqs[None, :]
    cos, sin = jnp.cos(ang), jnp.sin(ang)
    pad = HEAD_DIM - 2 * half
    t = pos.shape[0]
    c64 = jnp.concatenate([cos, cos, jnp.ones((t, pad), jnp.float32)], axis=1)
    s64 = jnp.concatenate([-sin, sin, jnp.zeros((t, pad), jnp.float32)], axis=1)
    return jnp.tile(c64, (1, 2)), jnp.tile(s64, (1, 2))


def _state_to_pairs(s):
    b = s.shape[0]
    s = s.reshape(b, N_PAIRS, 2, RET_DK, RET_DK)
    z = jnp.zeros_like(s[:, :, 0])
    top = jnp.concatenate([s[:, :, 0], z], axis=-1)
    bot = jnp.concatenate([z, s[:, :, 1]], axis=-1)
    return jnp.concatenate([top, bot], axis=-2)


def _pairs_to_state(sp):
    b = sp.shape[0]
    a = sp[:, :, :RET_DK, :RET_DK]
    d = sp[:, :, RET_DK:, RET_DK:]
    return jnp.stack([a, d], axis=2).reshape(b, H_RET, RET_DK, RET_DK)


def _mixer(x, pos, chunk, tb, g, win, wout, sinks, state=None, cache_k=None, cache_v=None):
    prompt = state is None
    b, t, _ = x.shape
    nb = t // tb
    cr, sr = _rope_tables(pos, RET_DK // 2, RET_THETA)
    ca, sa = _rope_tables(pos, ROPE_DIM // 2, ROPE_THETA)
    dmat, dq, dk, gdm = _retention_tables(tb, chunk)
    ctx = -(-(WINDOW + tb) // LANES) * LANES
    kpos = jnp.arange(ctx)[None, :]
    if prompt:
        qi = jnp.arange(tb)[:, None] // chunk
        kj = (kpos - WINDOW) // chunk
        band = (kj <= qi) & (kj >= qi - WINDOW // chunk) & (kpos < WINDOW + tb)
        first = band & (kpos >= WINDOW)
        bias = jnp.where(jnp.stack([first, band]), 0.0, NEG_INF).astype(jnp.float32)
        bias_map = lambda bi, ni: (jnp.minimum(ni, 1), 0, 0)
    else:
        valid = jnp.broadcast_to(kpos < WINDOW + tb, (1, tb, ctx))
        bias = jnp.where(valid, 0.0, NEG_INF).astype(jnp.float32)
        bias_map = lambda bi, ni: (0, 0, 0)
    rows = max(tb, LANES)

    full = lambda shape: pl.BlockSpec(shape, lambda bi, ni: (0,) * len(shape))
    in_specs = [
        pl.BlockSpec(memory_space=pltpu.SMEM),
        pl.BlockSpec((1, tb, D_MODEL), lambda bi, ni: (bi, ni, 0)),
        full((1, D_MODEL)),
        full((D_MODEL, IN_WIDTH)),
        full((D_MODEL, D_MODEL)),
        pl.BlockSpec((tb, LANES), lambda bi, ni: (ni, 0)),
        pl.BlockSpec((tb, LANES), lambda bi, ni: (ni, 0)),
        pl.BlockSpec((tb, LANES), lambda bi, ni: (ni, 0)),
        pl.BlockSpec((tb, LANES), lambda bi, ni: (ni, 0)),
        full((H_RET, tb, rows)),
        full((tb, 512)),
        full((tb, 512)),
        full((N_PAIRS, LANES, LANES)),
        pl.BlockSpec((1, tb, ctx), bias_map),
    ]
    args = [sinks, x, g.reshape(1, D_MODEL), win, wout, cr, sr, ca, sa, dmat, dq, dk, gdm, bias]
    scratch = []
    if prompt:
        scratch = [pltpu.VMEM((N_PAIRS, LANES, LANES), jnp.float32),
                   pltpu.VMEM((WINDOW, LANES), jnp.float32),
                   pltpu.VMEM((WINDOW, LANES), jnp.float32)]
    else:
        in_specs += [
            pl.BlockSpec((1, N_PAIRS, LANES, LANES), lambda bi, ni: (bi, 0, 0, 0)),
            pl.BlockSpec((1, WINDOW, LANES), lambda bi, ni: (bi, 0, 0)),
            pl.BlockSpec((1, WINDOW, LANES), lambda bi, ni: (bi, 0, 0)),
        ]
        args += [_state_to_pairs(state), cache_k, cache_v]
    out_shape = (
        jax.ShapeDtypeStruct((b, t, D_MODEL), jnp.float32),
        jax.ShapeDtypeStruct((b, N_PAIRS, LANES, LANES), jnp.float32),
        jax.ShapeDtypeStruct((b, WINDOW, LANES), jnp.float32),
        jax.ShapeDtypeStruct((b, WINDOW, LANES), jnp.float32),
    )
    out_specs = (
        pl.BlockSpec((1, tb, D_MODEL), lambda bi, ni: (bi, ni, 0)),
        pl.BlockSpec((1, N_PAIRS, LANES, LANES), lambda bi, ni: (bi, 0, 0, 0)),
        pl.BlockSpec((1, WINDOW, LANES), lambda bi, ni: (bi, 0, 0)),
        pl.BlockSpec((1, WINDOW, LANES), lambda bi, ni: (bi, 0, 0)),
    )
    x1, sp, kc, vc = pl.pallas_call(
        functools.partial(_mixer_kernel, prompt, tb),
        grid=(b, nb),
        in_specs=in_specs,
        out_specs=out_specs,
        out_shape=out_shape,
        scratch_shapes=scratch,
        compiler_params=pltpu.CompilerParams(
            dimension_semantics=("arbitrary", "arbitrary"),
            vmem_limit_bytes=48 * 1024 * 1024),
        name="mixer_prompt" if prompt else "mixer_sample",
    )(*args)
    return x1, _pairs_to_state(sp), kc, vc


def _top_rows(s, k, ids, payload=None):
    big = jnp.int32(2 ** 30)
    vals, win, pay = [], [], []
    for _ in range(k):
        m = jnp.max(s, axis=0, keepdims=True)
        w = jnp.min(jnp.where(s == m, ids, big), axis=0, keepdims=True)
        hit = ids == w
        vals.append(m)
        win.append(w)
        if payload is not None:
            pay.append(jnp.max(jnp.where(hit, payload, -1), axis=0, keepdims=True))
        s = jnp.where(hit, -jnp.inf, s)
    out = (jnp.concatenate(vals, axis=0), jnp.concatenate(win, axis=0))
    if payload is not None:
        out = out + (jnp.concatenate(pay, axis=0),)
    return out


def _route_kernel(x_ref, g_ref, wq_ref, keys_ref, h_ref, idx_ref, gate_ref):
    tb = x_ref.shape[0]
    x = x_ref[...]
    ms = jnp.mean(x * x, axis=-1, keepdims=True)
    h = x * lax.rsqrt(ms + NORM_EPS) * g_ref[...]
    h_ref[...] = h
    qt = _dot_nt(wq_ref[...], _bf(h))
    key_ids = lax.broadcasted_iota(jnp.int32, (PEER_NKEYS, tb), 0)
    sub = lax.broadcasted_iota(jnp.int32, (SUBLANES, tb), 0)
    e_rows, g_rows = [], []
    for p in range(PEER_HEADS):
        tops = []
        for s in range(2):
            ps = 2 * p + s
            st = _dot(keys_ref[ps], _bf(qt[ps * PEER_DKH:(ps + 1) * PEER_DKH]))
            tops.append(_top_rows(st, PEER_TOPK, key_ids))
        (s1, i1), (s2, i2) = tops
        c_val, c_id, c_exp = [], [], []
        for a in range(SUBLANES):
            nb = PEER_TOPK // (a + 1)
            for blk in range(2 if a == 0 else 1):
                b0 = blk * SUBLANES
                v = s1[a:a + 1] + s2[b0:b0 + SUBLANES]
                if nb < b0 + SUBLANES:
                    v = jnp.where(sub < nb - b0, v, -jnp.inf)
                c_val.append(v)
                c_id.append(sub + (a * PEER_TOPK + b0))
                c_exp.append(i1[a:a + 1] * PEER_NKEYS + i2[b0:b0 + SUBLANES])
        c_val.append(s1[SUBLANES:] + s2[0:1])
        c_id.append((sub + SUBLANES) * PEER_TOPK)
        c_exp.append(i1[SUBLANES:] * PEER_NKEYS + i2[0:1])
        sc, _, ex = _top_rows(jnp.concatenate(c_val, axis=0), PEER_TOPK,
                              jnp.concatenate(c_id, axis=0), jnp.concatenate(c_exp, axis=0))
        w = jnp.exp(sc - sc[0:1])
        g_rows.append(w / jnp.sum(w, axis=0, keepdims=True))
        e_rows.append(ex * ROWS_PER_EXPERT)
    idx_ref[...] = jnp.concatenate(e_rows, axis=0).T
    gate_ref[...] = jnp.concatenate(g_rows, axis=0).T


def _route(x, g, wq_t, keys):
    t = x.shape[0]
    tb = ROUTE_BLOCK
    return pl.pallas_call(
        _route_kernel,
        grid=(t // tb,),
        in_specs=[
            pl.BlockSpec((tb, D_MODEL), lambda i: (i, 0)),
            pl.BlockSpec((1, D_MODEL), lambda i: (0, 0)),
            pl.BlockSpec((2 * PEER_HEADS * PEER_DKH, D_MODEL), lambda i: (0, 0)),
            pl.BlockSpec((2 * PEER_HEADS, PEER_NKEYS, PEER_DKH), lambda i: (0, 0, 0)),
        ],
        out_specs=(
            pl.BlockSpec((tb, D_MODEL), lambda i: (i, 0)),
            pl.BlockSpec((tb, SEL), lambda i: (i, 0)),
            pl.BlockSpec((tb, SEL), lambda i: (i, 0)),
        ),
        out_shape=(
            jax.ShapeDtypeStruct((t, D_MODEL), jnp.float32),
            jax.ShapeDtypeStruct((t, SEL), jnp.int32),
            jax.ShapeDtypeStruct((t, SEL), jnp.float32),
        ),
        compiler_params=pltpu.CompilerParams(
            dimension_semantics=("arbitrary",),
            vmem_limit_bytes=48 * 1024 * 1024),
        name="peer_route",
    )(x, g.reshape(1, D_MODEL), wq_t, keys)


def _pack_table(tab):
    t = _bf(tab)
    lo = lax.bitcast_convert_type(t[:, :D_MODEL // 2], jnp.uint16).astype(jnp.uint32)
    hi = lax.bitcast_convert_type(t[:, D_MODEL // 2:], jnp.uint16).astype(jnp.uint32)
    return (lo | (hi << 16)).reshape(PEER_EXPERTS * ROWS_PER_EXPERT, LANES)


def _unpack(w):
    lo = lax.bitcast_convert_type(w << 16, jnp.float32)
    hi = lax.bitcast_convert_type(w & jnp.uint32(0xFFFF0000), jnp.float32)
    return lo, hi


def _gather_rows(idx_ref, t, tab_ref, stage_ref, slot_of):
    for k in range(SEL):
        e = pl.multiple_of(idx_ref[t, k], ROWS_PER_EXPERT)
        stage_ref[pl.ds(ROWS_PER_EXPERT * slot_of(k), ROWS_PER_EXPERT), :] = (
            tab_ref[pl.ds(e, ROWS_PER_EXPERT), :])


_FOLD_SLOT = (0, 4, 2, 6, 1, 5, 3, 7)


def _peer_u_kernel(idx_ref, gate_ref, h_ref, tab_ref, w_ref, stage_ref, part_ref, act_ref):
    tb = h_ref.shape[0]
    sub = lax.broadcasted_iota(jnp.int32, (SUBLANES, LANES), 0)
    lane = lax.broadcasted_iota(jnp.int32, (SEL, LANES), 1)
    low4 = sub < 4
    m2 = (sub & 3) < 2
    m1 = (sub & 1) == 0
    ones = jnp.ones((LANES, LANES), jnp.bfloat16)

    def flush(t):
        tot = _dot_split(part_ref[...], ones)
        pltpu.store(act_ref, tot, mask=lane == t)

    def token(t, carry):
        flush(t - 1)
        hv = h_ref[t]
        hr = pltpu.roll(hv, 4, 0)
        h_lo = jnp.where(low4, hv, hr)
        h_hi = jnp.where(low4, hr, hv)
        _gather_rows(idx_ref, t, tab_ref, stage_ref, lambda k: 8 * (k // 8) + _FOLD_SLOT[k % 8])
        for g in range(SEL // 8):
            ps = []
            for i in range(4):
                lo, hi = _unpack(stage_ref[pl.ds(32 * g + 8 * i, 8), :])
                ps.append(lo * h_lo + hi * h_hi)
            qs = []
            for i in range(2):
                p0, p1 = ps[2 * i], ps[2 * i + 1]
                a = jnp.where(m2, p0, pltpu.roll(p1, 2, 0))
                b = jnp.where(m2, pltpu.roll(p0, 6, 0), p1)
                qs.append(a + b)
            a = jnp.where(m1, qs[0], pltpu.roll(qs[1], 1, 0))
            b = jnp.where(m1, pltpu.roll(qs[0], 7, 0), qs[1])
            part_ref[pl.ds(8 * g, 8), :] = a + b
        return carry

    part_ref[...] = jnp.zeros_like(part_ref)
    lax.fori_loop(0, tb, token, 0)
    flush(tb - 1)
    act = act_ref[...].T
    w_ref[...] = gate_ref[...] * jax.nn.gelu(act)


def _peer_u(idx, gate, h_tiles, tab):
    t = idx.shape[0]
    tb = GATHER_BLOCK
    return pl.pallas_call(
        _peer_u_kernel,
        grid=(t // tb,),
        in_specs=[
            pl.BlockSpec((tb, SEL), lambda i: (i, 0), memory_space=pltpu.SMEM),
            pl.BlockSpec((tb, SEL), lambda i: (i, 0)),
            pl.BlockSpec((tb, SUBLANES, LANES), lambda i: (i, 0, 0)),
            pl.BlockSpec(memory_space=pltpu.VMEM),
        ],
        out_specs=pl.BlockSpec((tb, SEL), lambda i: (i, 0)),
        out_shape=jax.ShapeDtypeStruct((t, SEL), jnp.float32),
        scratch_shapes=[
            pltpu.VMEM((SEL * ROWS_PER_EXPERT, LANES), jnp.uint32),
            pltpu.VMEM((SEL, LANES), jnp.float32),
            pltpu.VMEM((SEL, tb), jnp.float32),
        ],
        compiler_params=pltpu.CompilerParams(
            dimension_semantics=("arbitrary",),
            vmem_limit_bytes=VMEM_LIMIT_TABLE),
        name="peer_u",
    )(idx, gate, h_tiles, tab)


def _peer_v_kernel(idx_ref, w_ref, x_ref, gfin_ref, tab_ref, y_ref, stage_ref):
    tb = x_ref.shape[0]
    sub = lax.broadcasted_iota(jnp.int32, (SUBLANES, LANES), 0)
    low4 = sub < 4
    n_acc = 4

    def token(t, carry):
        _gather_rows(idx_ref, t, tab_ref, stage_ref, lambda k: k)
        acc_lo = [jnp.zeros((SUBLANES, LANES), jnp.float32) for _ in range(n_acc)]
        acc_hi = [jnp.zeros((SUBLANES, LANES), jnp.float32) for _ in range(n_acc)]
        for j in range(SEL // 2):
            lo, hi = _unpack(stage_ref[pl.ds(8 * j, 8), :])
            wv = jnp.where(low4, w_ref[t, 2 * j], w_ref[t, 2 * j + 1])
            acc_lo[j % n_acc] = acc_lo[j % n_acc] + wv * lo
            acc_hi[j % n_acc] = acc_hi[j % n_acc] + wv * hi
        lo = (acc_lo[0] + acc_lo[1]) + (acc_lo[2] + acc_lo[3])
        hi = (acc_hi[0] + acc_hi[1]) + (acc_hi[2] + acc_hi[3])
        lo = lo + pltpu.roll(lo, 4, 0)
        hi = hi + pltpu.roll(hi, 4, 0)
        y_ref[t] = x_ref[t] + jnp.where(low4, lo, hi)
        return carry

    lax.fori_loop(0, tb, token, 0)
    y = y_ref[...]
    ss = jnp.sum(jnp.sum(y * y, axis=2, keepdims=True), axis=1, keepdims=True)
    y_ref[...] = y * lax.rsqrt(ss * (1.0 / D_MODEL) + NORM_EPS) * gfin_ref[...]


def _peer_v(idx, w, x_tiles, gfin, tab):
    t = idx.shape[0]
    tb = GATHER_BLOCK
    return pl.pallas_call(
        _peer_v_kernel,
        grid=(t // tb,),
        in_specs=[
            pl.BlockSpec((tb, SEL), lambda i: (i, 0), memory_space=pltpu.SMEM),
            pl.BlockSpec((tb, SEL), lambda i: (i, 0), memory_space=pltpu.SMEM),
            pl.BlockSpec((tb, SUBLANES, LANES), lambda i: (i, 0, 0)),
            pl.BlockSpec((1, SUBLANES, LANES), lambda i: (0, 0, 0)),
            pl.BlockSpec(memory_space=pltpu.VMEM),
        ],
        out_specs=pl.BlockSpec((tb, SUBLANES, LANES), lambda i: (i, 0, 0)),
        out_shape=jax.ShapeDtypeStruct((t, SUBLANES, LANES), jnp.float32),
        scratch_shapes=[pltpu.VMEM((SEL * ROWS_PER_EXPERT, LANES), jnp.uint32)],
        compiler_params=pltpu.CompilerParams(
            dimension_semantics=("arbitrary",),
            vmem_limit_bytes=VMEM_LIMIT_TABLE),
        name="peer_v",
    )(idx, w, x_tiles, gfin.reshape(1, SUBLANES, LANES), tab)


def _peer_and_final(x_flat, g_ffn, wq, keys, u_tab, v_tab, g_final):
    t = x_flat.shape[0]
    wq_t = _bf(wq.T)
    keys_bf = _bf(keys.reshape(2 * PEER_HEADS, PEER_NKEYS, PEER_DKH))
    h, idx, gate = _route(x_flat, g_ffn, wq_t, keys_bf)
    w = _peer_u(idx, gate, h.reshape(t, SUBLANES, LANES), _pack_table(u_tab))
    y = _peer_v(idx, w, x_flat.reshape(t, SUBLANES, LANES), g_final, _pack_table(v_tab))
    return y.reshape(t, D_MODEL)


def kernel(x_prompt, x_sample, state_ret, cache_swa_k, cache_swa_v, norm_mix, w_in, attn_sinks, w_out, norm_ffn, peer_w_q, peer_keys, peer_u, peer_v, norm_final):
    bp, tp, _ = x_prompt.shape
    bs, ts, _ = x_sample.shape
    win = _bf(w_in[0])
    wout = _bf(w_out[0])
    pos_p = jnp.arange(tp, dtype=jnp.float32)
    pos_s = jnp.arange(ts, dtype=jnp.float32) + PAST_LEN
    xp1, ret_p, kp, vp = _mixer(x_prompt, pos_p, CHUNK, MIX_BLOCK, norm_mix[0], win, wout, attn_sinks[0])
    xs1, ret_s, ks, vs = _mixer(
        x_sample, pos_s, ts, ts, norm_mix[0], win, wout, attn_sinks[0], state=state_ret[0],
        cache_k=cache_swa_k[0].reshape(bs, WINDOW, LANES), cache_v=cache_swa_v[0].reshape(bs, WINDOW, LANES))
    x_flat = jnp.concatenate([xp1.reshape(bp * tp, D_MODEL), xs1.reshape(bs * ts, D_MODEL)], axis=0)
    y = _peer_and_final(x_flat, norm_ffn[0], peer_w_q[0], peer_keys[0], peer_u[0], peer_v[0], norm_final)
    y_prompt = y[:bp * tp].reshape(bp, tp, D_MODEL)
    y_sample = y[bp * tp:].reshape(bs, ts, D_MODEL)
    cache_shape = (WINDOW, N_KV_HEADS, HEAD_DIM)
    return (y_prompt, y_sample, ret_p[None],
            kp.reshape((1, bp) + cache_shape), vp.reshape((1, bp) + cache_shape),
            ret_s[None],
            ks.reshape((1, bs) + cache_shape), vs.reshape((1, bs) + cache_shape))
```

```python
import functools

import numpy as np
import jax
import jax.numpy as jnp
from jax import lax
from jax.experimental import pallas as pl
from jax.experimental.pallas import tpu as pltpu

D_MODEL = 1024
CHUNK = 64
H_RET = 8
RET_DK = 64
RET_THETA = 10000.0
N_Q_HEADS = 8
N_KV_HEADS = 2
HEAD_DIM = 64
WINDOW = 128
ROPE_THETA = 500000.0
ROPE_DIM = HEAD_DIM // 4
PEER_HEADS = 8
PEER_NKEYS = 128
PEER_EXPERTS = PEER_NKEYS * PEER_NKEYS
PEER_TOPK = 16
PEER_DKH = 128
NORM_EPS = 1e-6
NEG_INF = -1e30
PAST_LEN = 4096

LANES = 128
SUBLANES = 8
VMEM_LIMIT_TABLE = 56 * 1024 * 1024

N_PAIRS = H_RET // 2
OFF_QR, OFF_KR, OFF_VR, OFF_GR = 0, 512, 1024, 1536
OFF_QA, OFF_KA, OFF_VA = 2048, 2560, 2688
IN_WIDTH = 2816
ROWS_PER_EXPERT = D_MODEL // 2 // LANES
SEL = PEER_HEADS * PEER_TOPK

MIX_BLOCK = 256
ROUTE_BLOCK = 128
GATHER_BLOCK = 128


def _bf(x):
    return x.astype(jnp.bfloat16)


def _dot(a, b):
    return jnp.dot(a, b, preferred_element_type=jnp.float32)


def _dot_nt(a, b):
    return lax.dot_general(a, b, (((1,), (1,)), ((), ())), preferred_element_type=jnp.float32)


def _dot_tn(a, b):
    return lax.dot_general(a, b, (((0,), (0,)), ((), ())), preferred_element_type=jnp.float32)


def _dot_split(a, m_bf):
    hi = _bf(a)
    lo = _bf(a - hi.astype(jnp.float32))
    return _dot(hi, m_bf) + _dot(lo, m_bf)


def _rope_cols(x, cos, sin, half):
    t = x.shape[0]
    lane = lax.broadcasted_iota(jnp.int32, (t, LANES), 1)
    first = (lane & (HEAD_DIM - 1)) < half
    cols = []
    for c in range(x.shape[1] // LANES):
        xs = x[:, c * LANES:(c + 1) * LANES]
        partner = jnp.where(first, pltpu.roll(xs, LANES - half, 1), pltpu.roll(xs, half, 1))
        cols.append(xs * cos + partner * sin)
    return cols


def _mixer_kernel(prompt, tb, *refs):
    if prompt:
        (sink_ref, x_ref, g_ref, win_ref, wout_ref, cr_ref, sr_ref, ca_ref, sa_ref, dmat_ref,
         dq_ref, dk_ref, gdm_ref, bias_ref,
         x1_ref, s_out_ref, kc_ref, vc_ref, s_scr, kprev_scr, vprev_scr) = refs
    else:
        (sink_ref, x_ref, g_ref, win_ref, wout_ref, cr_ref, sr_ref, ca_ref, sa_ref, dmat_ref,
         dq_ref, dk_ref, gdm_ref, bias_ref, s0_ref, ck_ref, cv_ref,
         x1_ref, s_out_ref, kc_ref, vc_ref) = refs

    if prompt:
        @pl.when(pl.program_id(1) == 0)
        def _():
            s_scr[...] = jnp.zeros_like(s_scr)
            kprev_scr[...] = jnp.zeros_like(kprev_scr)
            vprev_scr[...] = jnp.zeros_like(vprev_scr)
        state_ref = s_scr
        kprev = kprev_scr[...]
        vprev = vprev_scr[...]
    else:
        state_ref = s0_ref.at[0]
        kprev = ck_ref[0]
        vprev = cv_ref[0]

    x = x_ref[0]
    ms = jnp.mean(x * x, axis=-1, keepdims=True)
    h = _bf(x * lax.rsqrt(ms + NORM_EPS) * g_ref[...])

    def proj(off, width):
        return _dot(h, win_ref[:, off:off + width])

    lane = lax.broadcasted_iota(jnp.int32, (tb, LANES), 1)
    half_masks = (lane < HEAD_DIM, lane >= HEAD_DIM)
    row = lax.broadcasted_iota(jnp.int32, (LANES, LANES), 0)
    col = lax.broadcasted_iota(jnp.int32, (LANES, LANES), 1)
    same_head = (row < HEAD_DIM) == (col < HEAD_DIM)
    group_mean = _bf(jnp.where(same_head, 1.0 / HEAD_DIM, 0.0))

    cr, sr = cr_ref[...], sr_ref[...]
    q_cols = _rope_cols(proj(OFF_QR, 512), cr, sr, RET_DK // 2)
    k_cols = _rope_cols(proj(OFF_KR, 512), cr, sr, RET_DK // 2)
    v_all = proj(OFF_VR, 512)
    g_all = proj(OFF_GR, 512)
    merged = []
    rows = max(tb, LANES)

    def pad_rows(a):
        if rows == tb:
            return a
        return jnp.concatenate([a, jnp.zeros((rows - tb, a.shape[1]), a.dtype)], axis=0)

    for i in range(N_PAIRS):
        q = q_cols[i]
        k = pad_rows(k_cols[i] * (RET_DK ** -0.5))
        v = pad_rows(v_all[:, i * LANES:(i + 1) * LANES])
        kb = _bf(k)
        o = jnp.zeros((tb, LANES), jnp.float32)
        row_half = (lax.broadcasted_iota(jnp.int32, (rows, LANES), 1) < HEAD_DIM,
                    lax.broadcasted_iota(jnp.int32, (rows, LANES), 1) >= HEAD_DIM)
        for a in range(2):
            qm = _bf(jnp.where(half_masks[a], q, 0.0))
            sc = _dot_nt(qm, kb) * dmat_ref[2 * i + a]
            vm = _bf(jnp.where(row_half[a], v, 0.0))
            o = o + _dot(_bf(sc), vm)
        s_prev = state_ref[i]
        o = o + _dot(_bf(q), _bf(s_prev)) * dq_ref[:, i * LANES:(i + 1) * LANES]
        kd = k * pad_rows(dk_ref[:, i * LANES:(i + 1) * LANES])
        kv = _dot(_bf(kd.T), _bf(v))
        s_new = gdm_ref[i] * s_prev + jnp.where(same_head, kv, 0.0)
        if prompt:
            s_scr[i] = s_new
        s_out_ref[0, i] = s_new
        mu = _dot_split(o, group_mean)
        c = o - mu
        var = _dot_split(c * c, group_mean)
        on = c * lax.rsqrt(var + NORM_EPS)
        g = g_all[:, i * LANES:(i + 1) * LANES]
        merged.append(_bf(g * jax.nn.sigmoid(g) * on))

    ca, sa = ca_ref[...], sa_ref[...]
    qa_cols = _rope_cols(proj(OFF_QA, 512), ca, sa, ROPE_DIM // 2)
    ka = _rope_cols(proj(OFF_KA, 128), ca, sa, ROPE_DIM // 2)[0]
    va = proj(OFF_VA, 128)
    ctx = bias_ref.shape[2]
    tail = [jnp.zeros((ctx - WINDOW - tb, LANES), jnp.float32)] if ctx > WINDOW + tb else []
    k_ctx = jnp.concatenate([kprev, ka] + tail, axis=0)
    v_ctx = jnp.concatenate([vprev, va] + tail, axis=0)
    new_k = k_ctx[tb:tb + WINDOW]
    new_v = v_ctx[tb:tb + WINDOW]
    if prompt:
        kprev_scr[...] = new_k
        vprev_scr[...] = new_v
    kc_ref[0] = new_k
    vc_ref[0] = new_v
    k_both = (_bf(k_ctx), _bf(pltpu.roll(k_ctx, HEAD_DIM, 1)))
    v_sw = pltpu.roll(v_ctx, HEAD_DIM, 1)
    lane_c = lax.broadcasted_iota(jnp.int32, (ctx, LANES), 1)
    ctx_half = (lane_c < HEAD_DIM, lane_c >= HEAD_DIM)
    bias = bias_ref[0]
    for i in range(N_PAIRS):
        kvh = (2 * i) // (N_Q_HEADS // N_KV_HEADS)
        q = qa_cols[i]
        res = []
        for a in range(2):
            qm = _bf(jnp.where(half_masks[a], q, 0.0))
            kk = k_both[0] if kvh == a else k_both[1]
            s = _dot_nt(qm, kk) * (HEAD_DIM ** -0.5) + bias
            m = jnp.maximum(jnp.max(s, axis=-1, keepdims=True), sink_ref[2 * i + a])
            p = jnp.exp(s - m)
            vv = v_ctx if kvh == a else v_sw
            vext = _bf(jnp.where(ctx_half[a], vv, 1.0))
            r = _dot(_bf(p), vext)
            esink = jnp.exp(sink_ref[2 * i + a] - m)
            res.append(r + jnp.where(half_masks[a], 0.0, esink))
        num = jnp.where(half_masks[0], res[0], res[1])
        den = pltpu.roll(jnp.where(half_masks[0], res[1], res[0]), HEAD_DIM, 1)
        merged.append(_bf(num / den))

    cat = jnp.concatenate(merged, axis=1)
    x1_ref[0] = x + _dot(cat, wout_ref[...])


def _retention_tables(tb, chunk):
    hh = jnp.arange(H_RET, dtype=jnp.float32)
    log_g = jnp.log(1.0 - jnp.exp2(-5.0 - hh))
    idx = jnp.arange(tb, dtype=jnp.float32)
    dist = jnp.abs(idx[:, None] - idx[None, :])
    ci = jnp.arange(tb) // chunk
    visible = ci[None, :] <= ci[:, None]
    dmat = jnp.where(visible[None], jnp.exp(log_g[:, None, None] * dist[None]), 0.0)
    dq = jnp.exp(log_g[None, :] * (idx[:, None] + 1.0))
    dk = jnp.exp(log_g[None, :] * (tb - 1.0 - idx[:, None]))
    dq = jnp.repeat(dq, RET_DK, axis=1)
    dk = jnp.repeat(dk, RET_DK, axis=1)
    gd = jnp.repeat(jnp.exp(log_g * tb), RET_DK).reshape(N_PAIRS, LANES)
    r = jnp.arange(LANES) // HEAD_DIM
    blockdiag = (r[:, None] == r[None, :])
    gdm = jnp.where(blockdiag[None], gd[:, :, None], 0.0)
    rows = max(tb, LANES)
    dmat = jnp.pad(dmat, ((0, 0), (0, 0), (0, rows - tb)))
    return dmat, dq, dk, gdm


def _rope_tables(pos, half, theta):
    freqs = theta ** (-jnp.arange(half, dtype=jnp.float32) / half)
    ang = pos[:, None] * freqs[None, :]
    cos, sin = jnp.cos(ang), jnp.sin(ang)
    pad = HEAD_DIM - 2 * half
    t = pos.shape[0]
    c64 = jnp.concatenate([cos, cos, jnp.ones((t, pad), jnp.float32)], axis=1)
    s64 = jnp.concatenate([-sin, sin, jnp.zeros((t, pad), jnp.float32)], axis=1)
    return jnp.tile(c64, (1, 2)), jnp.tile(s64, (1, 2))


def _state_to_pairs(s):
    b = s.shape[0]
    s = s.reshape(b, N_PAIRS, 2, RET_DK, RET_DK)
    z = jnp.zeros_like(s[:, :, 0])
    top = jnp.concatenate([s[:, :, 0], z], axis=-1)
    bot = jnp.concatenate([z, s[:, :, 1]], axis=-1)
    return jnp.concatenate([top, bot], axis=-2)


def _pairs_to_state(sp):
    b = sp.shape[0]
    a = sp[:, :, :RET_DK, :RET_DK]
    d = sp[:, :, RET_DK:, RET_DK:]
    return jnp.stack([a, d], axis=2).reshape(b, H_RET, RET_DK, RET_DK)


def _mixer(x, pos, chunk, tb, g, win, wout, sinks, state=None, cache_k=None, cache_v=None):
    prompt = state is None
    b, t, _ = x.shape
    nb = t // tb
    cr, sr = _rope_tables(pos, RET_DK // 2, RET_THETA)
    ca, sa = _rope_tables(pos, ROPE_DIM // 2, ROPE_THETA)
    dmat, dq, dk, gdm = _retention_tables(tb, chunk)
    ctx = -(-(WINDOW + tb) // LANES) * LANES
    kpos = jnp.arange(ctx)[None, :]
    if prompt:
        qi = jnp.arange(tb)[:, None] // chunk
        kj = (kpos - WINDOW) // chunk
        band = (kj <= qi) & (kj >= qi - WINDOW // chunk) & (kpos < WINDOW + tb)
        first = band & (kpos >= WINDOW)
        bias = jnp.where(jnp.stack([first, band]), 0.0, NEG_INF).astype(jnp.float32)
        bias_map = lambda bi, ni: (jnp.minimum(ni, 1), 0, 0)
    else:
        valid = jnp.broadcast_to(kpos < WINDOW + tb, (1, tb, ctx))
        bias = jnp.where(valid, 0.0, NEG_INF).astype(jnp.float32)
        bias_map = lambda bi, ni: (0, 0, 0)
    rows = max(tb, LANES)

    full = lambda shape: pl.BlockSpec(shape, lambda bi, ni: (0,) * len(shape))
    in_specs = [
        pl.BlockSpec(memory_space=pltpu.SMEM),
        pl.BlockSpec((1, tb, D_MODEL), lambda bi, ni: (bi, ni, 0)),
        full((1, D_MODEL)),
        full((D_MODEL, IN_WIDTH)),
        full((D_MODEL, D_MODEL)),
        pl.BlockSpec((tb, LANES), lambda bi, ni: (ni, 0)),
        pl.BlockSpec((tb, LANES), lambda bi, ni: (ni, 0)),
        pl.BlockSpec((tb, LANES), lambda bi, ni: (ni, 0)),
        pl.BlockSpec((tb, LANES), lambda bi, ni: (ni, 0)),
        full((H_RET, tb, rows)),
        full((tb, 512)),
        full((tb, 512)),
        full((N_PAIRS, LANES, LANES)),
        pl.BlockSpec((1, tb, ctx), bias_map),
    ]
    args = [sinks, x, g.reshape(1, D_MODEL), win, wout, cr, sr, ca, sa, dmat, dq, dk, gdm, bias]
    scratch = []
    if prompt:
        scratch = [pltpu.VMEM((N_PAIRS, LANES, LANES), jnp.float32),
                   pltpu.VMEM((WINDOW, LANES), jnp.float32),
                   pltpu.VMEM((WINDOW, LANES), jnp.float32)]
    else:
        in_specs += [
            pl.BlockSpec((1, N_PAIRS, LANES, LANES), lambda bi, ni: (bi, 0, 0, 0)),
            pl.BlockSpec((1, WINDOW, LANES), lambda bi, ni: (bi, 0, 0)),
            pl.BlockSpec((1, WINDOW, LANES), lambda bi, ni: (bi, 0, 0)),
        ]
        args += [_state_to_pairs(state), cache_k, cache_v]
    out_shape = (
        jax.ShapeDtypeStruct((b, t, D_MODEL), jnp.float32),
        jax.ShapeDtypeStruct((b, N_PAIRS, LANES, LANES), jnp.float32),
        jax.ShapeDtypeStruct((b, WINDOW, LANES), jnp.float32),
        jax.ShapeDtypeStruct((b, WINDOW, LANES), jnp.float32),
    )
    out_specs = (
        pl.BlockSpec((1, tb, D_MODEL), lambda bi, ni: (bi, ni, 0)),
        pl.BlockSpec((1, N_PAIRS, LANES, LANES), lambda bi, ni: (bi, 0, 0, 0)),
        pl.BlockSpec((1, WINDOW, LANES), lambda bi, ni: (bi, 0, 0)),
        pl.BlockSpec((1, WINDOW, LANES), lambda bi, ni: (bi, 0, 0)),
    )
    x1, sp, kc, vc = pl.pallas_call(
        functools.partial(_mixer_kernel, prompt, tb),
        grid=(b, nb),
        in_specs=in_specs,
        out_specs=out_specs,
        out_shape=out_shape,
        scratch_shapes=scratch,
        compiler_params=pltpu.CompilerParams(
            dimension_semantics=("arbitrary", "arbitrary"),
            vmem_limit_bytes=48 * 1024 * 1024),
        name="mixer_prompt" if prompt else "mixer_sample",
    )(*args)
    return x1, _pairs_to_state(sp), kc, vc


def _top_rows(s, k, ids, payload=None):
    big = jnp.int32(2 ** 30)
    vals, win, pay = [], [], []
    for _ in range(k):
        m = jnp.max(s, axis=0, keepdims=True)
        w = jnp.min(jnp.where(s == m, ids, big), axis=0, keepdims=True)
        hit = ids == w
        vals.append(m)
        win.append(w)
        if payload is not None:
            pay.append(jnp.max(jnp.where(hit, payload, -1), axis=0, keepdims=True))
        s = jnp.where(hit, -jnp.inf, s)
    out = (jnp.concatenate(vals, axis=0), jnp.concatenate(win, axis=0))
    if payload is not None:
        out = out + (jnp.concatenate(pay, axis=0),)
    return out


def _route_kernel(x_ref, g_ref, wq_ref, keys_ref, h_ref, idx_ref, gate_ref):
    tb = x_ref.shape[0]
    x = x_ref[...]
    ms = jnp.mean(x * x, axis=-1, keepdims=True)
    h = x * lax.rsqrt(ms + NORM_EPS) * g_ref[...]
    h_ref[...] = h
    qt = _dot_nt(wq_ref[...], _bf(h))
    key_ids = lax.broadcasted_iota(jnp.int32, (PEER_NKEYS, tb), 0)
    sub = lax.broadcasted_iota(jnp.int32, (SUBLANES, tb), 0)
    e_rows, g_rows = [], []
    for p in range(PEER_HEADS):
        tops = []
        for s in range(2):
            ps = 2 * p + s
            st = _dot(keys_ref[ps], _bf(qt[ps * PEER_DKH:(ps + 1) * PEER_DKH]))
            tops.append(_top_rows(st, PEER_TOPK, key_ids))
        (s1, i1), (s2, i2) = tops
        c_val, c_id, c_exp = [], [], []
        for a in range(SUBLANES):
            nb = PEER_TOPK // (a + 1)
            for blk in range(2 if a == 0 else 1):
                b0 = blk * SUBLANES
                v = s1[a:a + 1] + s2[b0:b0 + SUBLANES]
                if nb < b0 + SUBLANES:
                    v = jnp.where(sub < nb - b0, v, -jnp.inf)
                c_val.append(v)
                c_id.append(sub + (a * PEER_TOPK + b0))
                c_exp.append(i1[a:a + 1] * PEER_NKEYS + i2[b0:b0 + SUBLANES])
        c_val.append(s1[SUBLANES:] + s2[0:1])
        c_id.append((sub + SUBLANES) * PEER_TOPK)
        c_exp.append(i1[SUBLANES:] * PEER_NKEYS + i2[0:1])
        sc, _, ex = _top_rows(jnp.concatenate(c_val, axis=0), PEER_TOPK,
                              jnp.concatenate(c_id, axis=0), jnp.concatenate(c_exp, axis=0))
        w = jnp.exp(sc - sc[0:1])
        g_rows.append(w / jnp.sum(w, axis=0, keepdims=True))
        e_rows.append((ex + 1) * ROWS_PER_EXPERT)
    idx_ref[...] = jnp.concatenate(e_rows, axis=0).T
    gate_ref[...] = jnp.concatenate(g_rows, axis=0).T


def _route(x, g, wq_t, keys):
    t = x.shape[0]
    tb = ROUTE_BLOCK
    return pl.pallas_call(
        _route_kernel,
        grid=(t // tb,),
        in_specs=[
            pl.BlockSpec((tb, D_MODEL), lambda i: (i, 0)),
            pl.BlockSpec((1, D_MODEL), lambda i: (0, 0)),
            pl.BlockSpec((2 * PEER_HEADS * PEER_DKH, D_MODEL), lambda i: (0, 0)),
            pl.BlockSpec((2 * PEER_HEADS, PEER_NKEYS, PEER_DKH), lambda i: (0, 0, 0)),
        ],
        out_specs=(
            pl.BlockSpec((tb, D_MODEL), lambda i: (i, 0)),
            pl.BlockSpec((tb, SEL), lambda i: (i, 0)),
            pl.BlockSpec((tb, SEL), lambda i: (i, 0)),
        ),
        out_shape=(
            jax.ShapeDtypeStruct((t, D_MODEL), jnp.float32),
            jax.ShapeDtypeStruct((t, SEL), jnp.int32),
            jax.ShapeDtypeStruct((t, SEL), jnp.float32),
        ),
        compiler_params=pltpu.CompilerParams(
            dimension_semantics=("arbitrary",),
            vmem_limit_bytes=48 * 1024 * 1024),
        name="peer_route",
    )(x, g.reshape(1, D_MODEL), wq_t, keys)


def _pack_table(tab):
    t = _bf(tab)
    lo = lax.bitcast_convert_type(t[:, :D_MODEL // 2], jnp.uint16).astype(jnp.uint32)
    hi = lax.bitcast_convert_type(t[:, D_MODEL // 2:], jnp.uint16).astype(jnp.uint32)
    rows = (lo | (hi << 16)).reshape(PEER_EXPERTS * ROWS_PER_EXPERT, LANES)
    return jnp.pad(rows, ((ROWS_PER_EXPERT, ROWS_PER_EXPERT), (0, 0)))


def _unpack(w):
    lo = lax.bitcast_convert_type(w << 16, jnp.float32)
    hi = lax.bitcast_convert_type(w & jnp.uint32(0xFFFF0000), jnp.float32)
    return lo, hi


def _gather_pair(idx_ref, t, tab_ref, ka, kb, low4):
    ea = pl.multiple_of(idx_ref[t, ka], ROWS_PER_EXPERT)
    eb = pl.multiple_of(idx_ref[t, kb], ROWS_PER_EXPERT)
    ra = tab_ref[pl.ds(ea, SUBLANES), :]
    rb = tab_ref[pl.ds(eb - ROWS_PER_EXPERT, SUBLANES), :]
    return jnp.where(low4, ra, rb)


K_TILE = 256
ROWS_PER_TOKEN = SEL * SUBLANES


def _gather_tile(idx_ref, t, tab_ref, q, low4):
    base = q * (K_TILE // SUBLANES)
    pairs = [_gather_pair(idx_ref, t, tab_ref, base + 2 * j, base + 2 * j + 1, low4)
             for j in range(K_TILE // 16)]
    return pltpu.bitcast(jnp.concatenate(pairs, axis=0), jnp.bfloat16)


TOKENS_PER_ITER = 4


def _for_tokens(tb, token):
    def trip(i, carry):
        for u in range(TOKENS_PER_ITER):
            token(i * TOKENS_PER_ITER + u, carry)
        return carry
    lax.fori_loop(0, tb // TOKENS_PER_ITER, trip, 0)


def _chunk_mask(width):
    c = lax.broadcasted_iota(jnp.int32, (SUBLANES, width), 0)
    i = lax.broadcasted_iota(jnp.int32, (SUBLANES, width), 1) & (SUBLANES - 1)
    return c == (i >> 1) + 4 * (i & 1)


def _expert_of_row():
    r = np.arange(ROWS_PER_TOKEN) // SUBLANES
    return jnp.asarray(r[None, :] == np.arange(SEL)[:, None], jnp.bfloat16)


def _peer_u_kernel(idx_ref, gate_ref, h_ref, tab_ref, et_ref, w_ref, s_ref):
    tb = h_ref.shape[0]
    low4 = lax.broadcasted_iota(jnp.int32, (SUBLANES, LANES), 0) < 4
    cmask = _chunk_mask(K_TILE)
    zeros = jnp.zeros((SUBLANES, LANES), jnp.float32)

    def token(t, carry):
        hb = _bf(jnp.concatenate([h_ref[t], zeros], axis=0))
        for q in range(ROWS_PER_TOKEN // K_TILE):
            m = _gather_tile(idx_ref, t, tab_ref, q, low4)
            s = _dot_nt(hb, m)
            s_ref[pl.ds(pl.multiple_of(t * SUBLANES, SUBLANES), SUBLANES),
                  q * K_TILE:(q + 1) * K_TILE] = jnp.where(cmask, s[:SUBLANES], 0.0)
        return carry

    _for_tokens(tb, token)
    part = _dot_split(s_ref[...], et_ref[...])
    act = jnp.sum(part.reshape(tb, SUBLANES, SEL), axis=1)
    w_ref[...] = gate_ref[...] * jax.nn.gelu(act)


def _peer_u(idx, gate, h_tiles, tab):
    t = idx.shape[0]
    tb = GATHER_BLOCK
    return pl.pallas_call(
        _peer_u_kernel,
        grid=(t // tb,),
        in_specs=[
            pl.BlockSpec((tb, SEL), lambda i: (i, 0), memory_space=pltpu.SMEM),
            pl.BlockSpec((tb, SEL), lambda i: (i, 0)),
            pl.BlockSpec((tb, SUBLANES, LANES), lambda i: (i, 0, 0)),
            pl.BlockSpec(memory_space=pltpu.VMEM),
            pl.BlockSpec((ROWS_PER_TOKEN, SEL), lambda i: (0, 0)),
        ],
        out_specs=pl.BlockSpec((tb, SEL), lambda i: (i, 0)),
        out_shape=jax.ShapeDtypeStruct((t, SEL), jnp.float32),
        scratch_shapes=[pltpu.VMEM((tb * SUBLANES, ROWS_PER_TOKEN), jnp.float32)],
        compiler_params=pltpu.CompilerParams(
            dimension_semantics=("arbitrary",),
            vmem_limit_bytes=VMEM_LIMIT_TABLE),
        name="peer_u",
    )(idx, gate, h_tiles, tab, _expert_of_row().T)


def _peer_v_kernel(idx_ref, w_ref, x_ref, gfin_ref, tab_ref, e_ref, y_ref, wexp_ref):
    tb = x_ref.shape[0]
    low4 = lax.broadcasted_iota(jnp.int32, (SUBLANES, LANES), 0) < 4
    cmask = _chunk_mask(ROWS_PER_TOKEN)
    wexp_ref[...] = _dot_split(w_ref[...], e_ref[...])

    def token(t, carry):
        wrow = jnp.broadcast_to(wexp_ref[pl.ds(t, 1), :], (SUBLANES, ROWS_PER_TOKEN))
        wm = jnp.where(cmask, wrow, 0.0)
        hi = _bf(wm).astype(jnp.float32)
        lhs = _bf(jnp.concatenate([hi, wm - hi], axis=0))
        acc = jnp.zeros((2 * SUBLANES, LANES), jnp.float32)
        for q in range(ROWS_PER_TOKEN // K_TILE):
            m = _gather_tile(idx_ref, t, tab_ref, q, low4)
            acc = acc + _dot(lhs[:, q * K_TILE:(q + 1) * K_TILE], m)
        y_ref[t] = x_ref[t] + (acc[:SUBLANES] + acc[SUBLANES:])
        return carry

    _for_tokens(tb, token)
    y = y_ref[...]
    ss = jnp.sum(jnp.sum(y * y, axis=2, keepdims=True), axis=1, keepdims=True)
    y_ref[...] = y * lax.rsqrt(ss * (1.0 / D_MODEL) + NORM_EPS) * gfin_ref[...]


def _peer_v(idx, w, x_tiles, gfin, tab):
    t = idx.shape[0]
    tb = GATHER_BLOCK
    return pl.pallas_call(
        _peer_v_kernel,
        grid=(t // tb,),
        in_specs=[
            pl.BlockSpec((tb, SEL), lambda i: (i, 0), memory_space=pltpu.SMEM),
            pl.BlockSpec((tb, SEL), lambda i: (i, 0)),
            pl.BlockSpec((tb, SUBLANES, LANES), lambda i: (i, 0, 0)),
            pl.BlockSpec((1, SUBLANES, LANES), lambda i: (0, 0, 0)),
            pl.BlockSpec(memory_space=pltpu.VMEM),
            pl.BlockSpec((SEL, ROWS_PER_TOKEN), lambda i: (0, 0)),
        ],
        out_specs=pl.BlockSpec((tb, SUBLANES, LANES), lambda i: (i, 0, 0)),
        out_shape=jax.ShapeDtypeStruct((t, SUBLANES, LANES), jnp.float32),
        scratch_shapes=[pltpu.VMEM((tb, ROWS_PER_TOKEN), jnp.float32)],
        compiler_params=pltpu.CompilerParams(
            dimension_semantics=("arbitrary",),
            vmem_limit_bytes=VMEM_LIMIT_TABLE),
        name="peer_v",
    )(idx, w, x_tiles, gfin.reshape(1, SUBLANES, LANES), tab, _expert_of_row())


def _peer_and_final(x_flat, g_ffn, wq, keys, u_tab, v_tab, g_final):
    t = x_flat.shape[0]
    wq_t = _bf(wq.T)
    keys_bf = _bf(keys.reshape(2 * PEER_HEADS, PEER_NKEYS, PEER_DKH))
    h, idx, gate = _route(x_flat, g_ffn, wq_t, keys_bf)
    w = _peer_u(idx, gate, h.reshape(t, SUBLANES, LANES), _pack_table(u_tab))
    y = _peer_v(idx, w, x_flat.reshape(t, SUBLANES, LANES), g_final, _pack_table(v_tab))
    return y.reshape(t, D_MODEL)


def kernel(x_prompt, x_sample, state_ret, cache_swa_k, cache_swa_v, norm_mix, w_in, attn_sinks, w_out, norm_ffn, peer_w_q, peer_keys, peer_u, peer_v, norm_final):
    bp, tp, _ = x_prompt.shape
    bs, ts, _ = x_sample.shape
    win = _bf(w_in[0])
    wout = _bf(w_out[0])
    pos_p = jnp.arange(tp, dtype=jnp.float32)
    pos_s = jnp.arange(ts, dtype=jnp.float32) + PAST_LEN
    xp1, ret_p, kp, vp = _mixer(x_prompt, pos_p, CHUNK, MIX_BLOCK, norm_mix[0], win, wout, attn_sinks[0])
    xs1, ret_s, ks, vs = _mixer(
        x_sample, pos_s, ts, ts, norm_mix[0], win, wout, attn_sinks[0], state=state_ret[0],
        cache_k=cache_swa_k[0].reshape(bs, WINDOW, LANES), cache_v=cache_swa_v[0].reshape(bs, WINDOW, LANES))
    x_flat = jnp.concatenate([xp1.reshape(bp * tp, D_MODEL), xs1.reshape(bs * ts, D_MODEL)], axis=0)
    y = _peer_and_final(x_flat, norm_ffn[0], peer_w_q[0], peer_keys[0], peer_u[0], peer_v[0], norm_final)
    y_prompt = y[:bp * tp].reshape(bp, tp, D_MODEL)
    y_sample = y[bp * tp:].reshape(bs, ts, D_MODEL)
    cache_shape = (WINDOW, N_KV_HEADS, HEAD_DIM)
    return (y_prompt, y_sample, ret_p[None],
            kp.reshape((1, bp) + cache_shape), vp.reshape((1, bp) + cache_shape),
            ret_s[None],
            ks.reshape((1, bs) + cache_shape), vs.reshape((1, bs) + cache_shape))
```
